```python
import jax, jax.numpy as jnp
from jax import lax
import numpy as np

D_MODEL = 4096
BATCH = 1
SEQ = 16384
DEPTH = 1

ATTN_HEADS = D_MODEL // 256
HEAD_DIM = 128
ATTN_WIDTH = ATTN_HEADS * HEAD_DIM
MOBA_BLOCK = 256
MOBA_TOPK = 3
MOBA_Q_CHUNK = 64
ROPE_THETA = 10000.0
SGU_GROUPS = D_MODEL // 256
SGU_GROUP_DIM = 128
SGU_WIDTH = SGU_GROUPS * SGU_GROUP_DIM
SGU_CHUNK = 128
IN_WIDTH = 3 * ATTN_WIDTH + 2 * SGU_WIDTH
N_BRANCHES = 2
PEER_HEADS = 8
PEER_N_KEYS = 128
PEER_N_EXPERTS = PEER_N_KEYS * PEER_N_KEYS
PEER_TOPK = 16
PEER_KEY_HALF = 128
PEER_TOKEN_CHUNK = 128
RMS_EPS = 1e-6

kernel_name = 'hybrid_moba_sgu_peer_block'


def rms_norm(x, g):
    xf = x.astype(jnp.float32)
    y = xf * lax.rsqrt(jnp.mean(xf * xf, axis=-1, keepdims=True) + RMS_EPS)
    return (y * g.astype(jnp.float32)).astype(x.dtype)


def rope_tables(seq):
    half = HEAD_DIM // 2
    inv_freq = jnp.power(ROPE_THETA, -jnp.arange(half, dtype=jnp.float32) * (2.0 / HEAD_DIM))
    ang = jnp.arange(seq, dtype=jnp.float32)[:, None] * inv_freq[None, :]
    return jnp.cos(ang), jnp.sin(ang)


def apply_rope(x, cos, sin):
    xf = x.astype(jnp.float32)
    x1, x2 = jnp.split(xf, 2, axis=-1)
    c = cos[None, :, None, :]
    s = sin[None, :, None, :]
    return jnp.concatenate([x1 * c - x2 * s, x2 * c + x1 * s], axis=-1).astype(x.dtype)


def moba_attention(q, k, v):
    B, S, H, Dh = q.shape
    nb = -(-S // MOBA_BLOCK)
    pad = nb * MOBA_BLOCK - S
    k = jnp.pad(k, ((0, 0), (0, pad), (0, 0), (0, 0)))
    v = jnp.pad(v, ((0, 0), (0, pad), (0, 0), (0, 0)))
    kb = k.reshape(B, nb, MOBA_BLOCK, H, Dh).transpose(0, 3, 1, 2, 4)
    vb = v.reshape(B, nb, MOBA_BLOCK, H, Dh).transpose(0, 3, 1, 2, 4)
    k_mean = jnp.mean(kb.astype(jnp.float32), axis=3).astype(k.dtype)
    n_sel = min(MOBA_TOPK, nb)
    nqc = S // MOBA_Q_CHUNK
    qc = q.reshape(B, nqc, MOBA_Q_CHUNK, H, Dh)
    scale = HEAD_DIM ** -0.5
    head_ix = jnp.arange(H)[None, :, None]

    def one_chunk(bc):
        b, c = bc
        qx = qc[b, c]
        kx = kb[b]
        vx = vb[b]
        own = (c * MOBA_Q_CHUNK) // MOBA_BLOCK
        gate = jnp.einsum('qhd,hnd->qhn', qx, k_mean[b]).astype(jnp.float32)
        gate = jnp.where((jnp.arange(nb) < own)[None, None, :], gate, -jnp.inf)
        _, sel = lax.top_k(gate, n_sel)
        valid = jnp.arange(n_sel) < own
        k_sel = kx[head_ix, sel]
        v_sel = vx[head_ix, sel]
        s_past = jnp.einsum('qhd,qhnsd->qhns', qx, k_sel).astype(jnp.float32) * scale
        s_past = jnp.where(valid[None, None, :, None], s_past, -jnp.inf)
        s_past = s_past.reshape(MOBA_Q_CHUNK, H, n_sel * MOBA_BLOCK)
        k_own = lax.dynamic_index_in_dim(kx, own, axis=1, keepdims=False)
        v_own = lax.dynamic_index_in_dim(vx, own, axis=1, keepdims=False)
        s_own = jnp.einsum('qhd,hsd->qhs', qx, k_own).astype(jnp.float32) * scale
        q_pos = c * MOBA_Q_CHUNK + jnp.arange(MOBA_Q_CHUNK)
        k_pos = own * MOBA_BLOCK + jnp.arange(MOBA_BLOCK)
        s_own = jnp.where((k_pos[None, :] <= q_pos[:, None])[:, None, :], s_own, -jnp.inf)
        p = jax.nn.softmax(jnp.concatenate([s_past, s_own], axis=-1), axis=-1).astype(v.dtype)
        p_past = p[..., :n_sel * MOBA_BLOCK].reshape(MOBA_Q_CHUNK, H, n_sel, MOBA_BLOCK)
        p_own = p[..., n_sel * MOBA_BLOCK:]
        return (jnp.einsum('qhns,qhnsd->qhd', p_past, v_sel)
                + jnp.einsum('qhs,hsd->qhd', p_own, v_own))

    b_ix = jnp.repeat(jnp.arange(B), nqc)
    c_ix = jnp.tile(jnp.arange(nqc), B)
    out = lax.map(one_chunk, (b_ix, c_ix))
    return out.reshape(B, S, H * Dh)


def spatial_gating(zu, zv, norm_g, w_s, b_s):
    B, S, _ = zu.shape
    u = jax.nn.gelu(zu)
    vv = rms_norm(jax.nn.gelu(zv).reshape(B, S, SGU_GROUPS, SGU_GROUP_DIM), norm_g)
    nc = S // SGU_CHUNK
    vv = vv.reshape(B, nc, SGU_CHUNK, SGU_GROUPS, SGU_GROUP_DIM)
    causal = jnp.tril(jnp.ones((SGU_CHUNK, SGU_CHUNK), dtype=w_s.dtype))
    mixed = jnp.einsum('gts,bcsgd->bctgd', w_s * causal[None], vv) + b_s.T[None, None, :, :, None]
    return u * mixed.reshape(B, S, SGU_WIDTH)


def peer_ffn(h, wq, subkeys, u_tab, v_tab):
    B, S, D = h.shape
    T = B * S
    ht = h.reshape(T, D)
    q = (ht @ wq).reshape(T, PEER_HEADS, 2, PEER_KEY_HALF)
    s = jnp.einsum('thpd,hpnd->thpn', q, subkeys).astype(jnp.float32)
    s_top, i_top = lax.top_k(s, PEER_TOPK)
    cand = s_top[:, :, 0, :, None] + s_top[:, :, 1, None, :]
    cand_idx = i_top[:, :, 0, :, None] * PEER_N_KEYS + i_top[:, :, 1, None, :]
    cand = cand.reshape(T, PEER_HEADS, PEER_TOPK * PEER_TOPK)
    cand_idx = cand_idx.reshape(T, PEER_HEADS, PEER_TOPK * PEER_TOPK)
    best, pos = lax.top_k(cand, PEER_TOPK)
    experts = jnp.take_along_axis(cand_idx, pos, axis=-1)
    gates = jax.nn.softmax(best, axis=-1)
    nch = T // PEER_TOKEN_CHUNK

    def one_chunk(args):
        hx, ex, gx = args
        u_sel = u_tab[ex]
        act = jax.nn.gelu(jnp.einsum('td,thkd->thk', hx, u_sel))
        w = gx.astype(hx.dtype) * act
        return jnp.einsum('thk,thkd->td', w, v_tab[ex])

    out = lax.map(one_chunk, (ht.reshape(nch, PEER_TOKEN_CHUNK, D),
                              experts.reshape(nch, PEER_TOKEN_CHUNK, PEER_HEADS, PEER_TOPK),
                              gates.reshape(nch, PEER_TOKEN_CHUNK, PEER_HEADS, PEER_TOPK)))
    return out.reshape(B, S, D)


def setup_inputs(seed: int = 0) -> dict:
    key = jax.random.key(seed)
    ks = jax.random.split(key, 17)
    L = DEPTH

    def nrm(k, shape, scale):
        return scale * jax.random.normal(k, shape, dtype=jnp.float32)

    return {
        'x': nrm(ks[0], (BATCH, SEQ, D_MODEL), 1.0),
        'norm_mix_g': 1.0 + nrm(ks[1], (L, D_MODEL), 0.01),
        'w_in': nrm(ks[2], (L, D_MODEL, IN_WIDTH), D_MODEL ** -0.5),
        'w_attn_proj': nrm(ks[3], (L, ATTN_WIDTH, D_MODEL), ATTN_WIDTH ** -0.5),
        'sgu_norm_g': 1.0 + nrm(ks[4], (L, SGU_GROUPS, SGU_GROUP_DIM), 0.01),
        'sgu_w': nrm(ks[5], (L, SGU_GROUPS, SGU_CHUNK, SGU_CHUNK), 0.5 * SGU_CHUNK ** -0.5),
        'sgu_b': 1.0 + nrm(ks[6], (L, SGU_GROUPS, SGU_CHUNK), 0.1),
        'w_sgu_proj': nrm(ks[7], (L, SGU_WIDTH, D_MODEL), SGU_WIDTH ** -0.5),
        'w_gate': nrm(ks[8], (L, D_MODEL, N_BRANCHES * D_MODEL), D_MODEL ** -0.5),
        'b_gate': nrm(ks[9], (L, N_BRANCHES * D_MODEL), 0.1),
        'w_o': nrm(ks[10], (L, D_MODEL, D_MODEL), D_MODEL ** -0.5),
        'norm_ffn_g': 1.0 + nrm(ks[11], (L, D_MODEL), 0.01),
        'peer_wq': nrm(ks[12], (L, D_MODEL, PEER_HEADS * 2 * PEER_KEY_HALF), D_MODEL ** -0.5),
        'peer_subkeys': nrm(ks[13], (L, PEER_HEADS, 2, PEER_N_KEYS, PEER_KEY_HALF), PEER_KEY_HALF ** -0.5),
        'peer_u': nrm(ks[14], (L, PEER_N_EXPERTS, D_MODEL), D_MODEL ** -0.5),
        'peer_v': nrm(ks[15], (L, PEER_N_EXPERTS, D_MODEL), 0.5),
        'norm_final_g': 1.0 + nrm(ks[16], (D_MODEL,), 0.01),
    }


def reference(x, norm_mix_g, w_in, w_attn_proj, sgu_norm_g, sgu_w, sgu_b, w_sgu_proj,
              w_gate, b_gate, w_o, norm_ffn_g, peer_wq, peer_subkeys, peer_u, peer_v,
              norm_final_g):
    B, S, _ = x.shape
    cos, sin = rope_tables(S)
    splits = [ATTN_WIDTH, 2 * ATTN_WIDTH, 3 * ATTN_WIDTH, 3 * ATTN_WIDTH + SGU_WIDTH]
    for l in range(DEPTH):
        h = rms_norm(x, norm_mix_g[l])
        zq, zk, zva, zu, zv = jnp.split(h @ w_in[l], splits, axis=-1)
        q = apply_rope(zq.reshape(B, S, ATTN_HEADS, HEAD_DIM), cos, sin)
        k = apply_rope(zk.reshape(B, S, ATTN_HEADS, HEAD_DIM), cos, sin)
        va = zva.reshape(B, S, ATTN_HEADS, HEAD_DIM)
        y_attn = moba_attention(q, k, va) @ w_attn_proj[l]
        y_sgu = spatial_gating(zu, zv, sgu_norm_g[l], sgu_w[l], sgu_b[l]) @ w_sgu_proj[l]
        g_attn, g_sgu = jnp.split(jax.nn.sigmoid(h @ w_gate[l] + b_gate[l]), N_BRANCHES, axis=-1)
        x = x + (g_attn * y_attn + g_sgu * y_sgu) @ w_o[l]
        h2 = rms_norm(x, norm_ffn_g[l])
        x = x + peer_ffn(h2, peer_wq[l], peer_subkeys[l], peer_u[l], peer_v[l])
    return rms_norm(x, norm_final_g)
```

```python
import functools

import jax
import jax.numpy as jnp
from jax import lax
from jax.experimental import pallas as pl
from jax.experimental.pallas import tpu as pltpu

F32, BF16, I32 = jnp.float32, jnp.bfloat16, jnp.int32

HEAD_DIM = 128
MOBA_BLOCK = 256
MOBA_TOPK = 3
ROPE_THETA = 10000.0
SGU_GROUP_DIM = 128
SGU_CHUNK = 128
PEER_HEADS = 8
PEER_N_KEYS = 128
PEER_TOPK = 16
PEER_KEY_HALF = 128
RMS_EPS = 1e-6

LANES = 128
VMEM_LIMIT_CAP = 60000 * 1024

MASK_NEG = -1e9
NT_DIMS = (((1,), (1,)), ((), ()))


def _vmem_limit(pipelined_bytes, resident_bytes=0):
    return int(min(VMEM_LIMIT_CAP, 2 * pipelined_bytes + resident_bytes + (8 << 20)))


def _params(sem, pipelined_bytes, resident_bytes=0):
    return pltpu.CompilerParams(dimension_semantics=sem,
                                vmem_limit_bytes=_vmem_limit(pipelined_bytes, resident_bytes))


def _rmsnorm_kernel(*refs, n_in):
    o_ref = refs[-1]
    g_ref = refs[n_in]
    x = refs[0][...]
    for r in refs[1:n_in]:
        x = x + r[...]
    ms = jnp.mean(x * x, axis=-1, keepdims=True)
    o_ref[...] = (x * lax.rsqrt(ms + RMS_EPS) * g_ref[...]).astype(o_ref.dtype)


def _rmsnorm(xs, g, out_dtype, tm=256):
    S, D = xs[0].shape
    spec = pl.BlockSpec((tm, D), lambda i: (i, 0))
    return pl.pallas_call(
        functools.partial(_rmsnorm_kernel, n_in=len(xs)),
        grid=(S // tm,),
        in_specs=[spec] * len(xs) + [pl.BlockSpec((1, D), lambda i: (0, 0))],
        out_specs=spec,
        out_shape=jax.ShapeDtypeStruct((S, D), out_dtype),
        compiler_params=_params(("parallel",), (len(xs) + 1) * tm * D * 4),
        name="rmsnorm",
    )(*xs, g.reshape(1, D).astype(F32))


def _add_kernel(a_ref, b_ref, o_ref):
    o_ref[...] = a_ref[...] + b_ref[...]


def _residual_add(a, b, tm=256):
    S, D = a.shape
    spec = pl.BlockSpec((tm, D), lambda i: (i, 0))
    return pl.pallas_call(
        _add_kernel, grid=(S // tm,), in_specs=[spec, spec], out_specs=spec,
        out_shape=jax.ShapeDtypeStruct((S, D), F32),
        compiler_params=_params(("parallel",), 3 * tm * D * 4),
        name="residual_add",
    )(a, b)


def _proj_kernel(h_ref, w_ref, cv_ref, cos_ref, sin_ref, o_ref, *, tn, bounds, q_scale):
    j = pl.program_id(1)
    acc = jnp.dot(h_ref[...], w_ref[...], preferred_element_type=F32)
    nslab = tn // LANES
    b_q, b_k, b_v, b_u, b_z = bounds

    def rope(scale):
        cos, sin = cos_ref[...], sin_ref[...]
        for s in range(nslab):
            xs = acc[:, s * LANES:(s + 1) * LANES]
            y = xs * cos + pltpu.roll(xs, HEAD_DIM // 2, 1) * sin
            if scale != 1.0:
                y = y * scale
            o_ref[:, s * LANES:(s + 1) * LANES] = y.astype(o_ref.dtype)

    @pl.when(j < b_q)
    def _():
        rope(q_scale)

    @pl.when((j >= b_q) & (j < b_k))
    def _():
        rope(1.0)

    @pl.when((j >= b_k) & (j < b_v))
    def _():
        o_ref[...] = acc.astype(o_ref.dtype)

    @pl.when((j >= b_v) & (j < b_u))
    def _():
        o_ref[...] = jax.nn.gelu(acc).astype(o_ref.dtype)

    @pl.when((j >= b_u) & (j < b_z))
    def _():
        gam = cv_ref[...]
        for s in range(nslab):
            gl = jax.nn.gelu(acc[:, s * LANES:(s + 1) * LANES])
            ms = jnp.mean(gl * gl, axis=-1, keepdims=True)
            y = gl * lax.rsqrt(ms + RMS_EPS) * gam[:, s * LANES:(s + 1) * LANES]
            o_ref[:, s * LANES:(s + 1) * LANES] = y.astype(o_ref.dtype)

    @pl.when(j >= b_z)
    def _():
        o_ref[...] = jax.nn.sigmoid(acc + cv_ref[...]).astype(o_ref.dtype)


def _fused_proj(h, w_cat, colvec, cosf, sinf, widths, q_scale, tm=1024, tn=512):
    S, D = h.shape
    N = w_cat.shape[1]
    bounds, acc = [], 0
    for w in widths[:5]:
        acc += w // tn
        bounds.append(acc)
    return pl.pallas_call(
        functools.partial(_proj_kernel, tn=tn, bounds=tuple(bounds), q_scale=q_scale),
        grid=(S // tm, N // tn),
        in_specs=[
            pl.BlockSpec((tm, D), lambda i, j: (i, 0)),
            pl.BlockSpec((D, tn), lambda i, j: (0, j)),
            pl.BlockSpec((1, tn), lambda i, j: (0, j)),
            pl.BlockSpec((tm, LANES), lambda i, j: (i, 0)),
            pl.BlockSpec((tm, LANES), lambda i, j: (i, 0)),
        ],
        out_specs=pl.BlockSpec((tm, tn), lambda i, j: (i, j)),
        out_shape=jax.ShapeDtypeStruct((S, N), BF16),
        compiler_params=_params(("parallel", "arbitrary"),
                                tm * D * 2 + D * tn * 2 + tm * tn * 2 + 2 * tm * LANES * 4,
                                3 * tm * tn * 4),
        name="fused_proj",
    )(h, w_cat, colvec, cosf, sinf)


def _kmean_kernel(k_ref, o_ref):
    o_ref[0] = jnp.mean(k_ref[...].astype(F32), axis=0, keepdims=True)


def _key_block_means(proj, k_col_block, width):
    S = proj.shape[0]
    nb = S // MOBA_BLOCK
    return pl.pallas_call(
        _kmean_kernel,
        grid=(nb,),
        in_specs=[pl.BlockSpec((MOBA_BLOCK, width), lambda b: (b, k_col_block))],
        out_specs=pl.BlockSpec((1, 1, width), lambda b: (b, 0, 0)),
        out_shape=jax.ShapeDtypeStruct((nb, 1, width), F32),
        compiler_params=_params(("parallel",), MOBA_BLOCK * width * 2 + width * 4),
        name="key_block_means",
    )(proj)


def _moba_kernel(q_ref, k_ref, v_ref, km_ref, o_ref, qaug_ref, m_ref, l_ref, acc_ref, *, tq):
    qi = pl.program_id(1)
    sub = tq // MOBA_BLOCK
    shift = MOBA_BLOCK.bit_length() - 1
    q = q_ref[...]

    gate = lax.dot_general(q, km_ref[0], NT_DIMS, preferred_element_type=F32)
    row = lax.broadcasted_iota(I32, (tq, LANES), 0)
    lane = lax.broadcasted_iota(I32, (tq, LANES), 1)
    own = qi * sub + (row >> shift)
    g = jnp.where(lane < own, gate, -jnp.inf)
    sel = lane == own
    for _ in range(MOBA_TOPK):
        m = jnp.max(g, axis=1, keepdims=True)
        cand = jnp.where((g == m) & (m > -jnp.inf), lane, LANES)
        idx = jnp.min(cand, axis=1, keepdims=True)
        hit = lane == idx
        sel = sel | hit
        g = jnp.where(hit, -jnp.inf, g)
    qaug_ref[:, :LANES] = q
    qaug_ref[:, LANES:] = jnp.where(sel, 0.0, MASK_NEG).astype(BF16)

    m_ref[...] = jnp.full(m_ref.shape, -jnp.inf, F32)
    l_ref[...] = jnp.zeros(l_ref.shape, F32)
    acc_ref[...] = jnp.zeros(acc_ref.shape, F32)

    krow = lax.broadcasted_iota(I32, (tq, LANES), 0)
    klane = lax.broadcasted_iota(I32, (tq, LANES), 1)

    def chunk(c, causal):
        start = pl.multiple_of(c * tq, tq)
        kc = k_ref[pl.ds(start, tq), :]
        vc = v_ref[pl.ds(start, tq), :]
        onehot = (klane == c * sub + (krow >> shift)).astype(BF16)
        kaug = jnp.concatenate([kc, onehot], axis=1)
        s = lax.dot_general(qaug_ref[...], kaug, NT_DIMS, preferred_element_type=F32)
        if causal:
            r2 = lax.broadcasted_iota(I32, (tq, tq), 0)
            c2 = lax.broadcasted_iota(I32, (tq, tq), 1)
            s = jnp.where(c2 <= r2, s, MASK_NEG)
        m_prev = m_ref[...]
        m_new = jnp.maximum(m_prev, jnp.max(s, axis=1, keepdims=True))
        alpha = jnp.exp(m_prev - m_new)
        p = jnp.exp(s - m_new)
        l_ref[...] = alpha * l_ref[...] + jnp.sum(p, axis=1, keepdims=True)
        acc_ref[...] = alpha * acc_ref[...] + jnp.dot(p.astype(BF16), vc, preferred_element_type=F32)
        m_ref[...] = m_new

    def body(c, carry):
        chunk(c, False)
        return carry

    lax.fori_loop(0, qi, body, 0)
    chunk(qi, True)
    o_ref[...] = (acc_ref[...] / l_ref[...]).astype(o_ref.dtype)


def _moba_attention(proj, kmean_pad, n_heads, tq=512):
    S = proj.shape[0]
    H = n_heads
    return pl.pallas_call(
        functools.partial(_moba_kernel, tq=tq),
        grid=(H, S // tq),
        in_specs=[
            pl.BlockSpec((tq, HEAD_DIM), lambda h, i: (i, h)),
            pl.BlockSpec((S, HEAD_DIM), lambda h, i: (0, H + h)),
            pl.BlockSpec((S, HEAD_DIM), lambda h, i: (0, 2 * H + h)),
            pl.BlockSpec((1, LANES, HEAD_DIM), lambda h, i: (h, 0, 0)),
        ],
        out_specs=pl.BlockSpec((tq, HEAD_DIM), lambda h, i: (i, h)),
        out_shape=jax.ShapeDtypeStruct((S, H * HEAD_DIM), BF16),
        scratch_shapes=[
            pltpu.VMEM((tq, 2 * LANES), BF16),
            pltpu.VMEM((tq, 1), F32),
            pltpu.VMEM((tq, 1), F32),
            pltpu.VMEM((tq, HEAD_DIM), F32),
        ],
        compiler_params=_params(("parallel", "arbitrary"),
                                2 * S * HEAD_DIM * 2 + 2 * tq * HEAD_DIM * 2 + LANES * HEAD_DIM * 2,
                                6 * tq * tq * 4),
        name="moba_attention",
    )(proj, proj, proj, kmean_pad)


def _sgu_kernel(u_ref, vv_ref, w_ref, b_ref, o_ref, wm_ref, *, groups):
    @pl.when(pl.program_id(0) == 0)
    def _():
        r = lax.broadcasted_iota(I32, (SGU_CHUNK, SGU_CHUNK), 0)
        c = lax.broadcasted_iota(I32, (SGU_CHUNK, SGU_CHUNK), 1)
        for g in range(groups):
            wm_ref[g] = jnp.where(c <= r, w_ref[g], 0.0).astype(BF16)

    for g in range(groups):
        sl = slice(g * SGU_GROUP_DIM, (g + 1) * SGU_GROUP_DIM)
        mixed = jnp.dot(wm_ref[g], vv_ref[:, sl], preferred_element_type=F32) + b_ref[:, sl]
        o_ref[:, sl] = (u_ref[:, sl].astype(F32) * mixed).astype(o_ref.dtype)


def _spatial_gating(proj, w_s, b_exp, u_col_block, v_col_block):
    S = proj.shape[0]
    G = w_s.shape[0]
    W = G * SGU_GROUP_DIM
    return pl.pallas_call(
        functools.partial(_sgu_kernel, groups=G),
        grid=(S // SGU_CHUNK,),
        in_specs=[
            pl.BlockSpec((SGU_CHUNK, W), lambda c: (c, u_col_block)),
            pl.BlockSpec((SGU_CHUNK, W), lambda c: (c, v_col_block)),
            pl.BlockSpec((G, SGU_CHUNK, SGU_CHUNK), lambda c: (0, 0, 0)),
            pl.BlockSpec((SGU_CHUNK, W), lambda c: (0, 0)),
        ],
        out_specs=pl.BlockSpec((SGU_CHUNK, W), lambda c: (c, 0)),
        out_shape=jax.ShapeDtypeStruct((S, W), BF16),
        scratch_shapes=[pltpu.VMEM((G, SGU_CHUNK, SGU_CHUNK), BF16)],
        compiler_params=_params(("arbitrary",),
                                3 * SGU_CHUNK * W * 2 + G * SGU_CHUNK * SGU_CHUNK * 4 + SGU_CHUNK * W * 4,
                                G * SGU_CHUNK * SGU_CHUNK * 2),
        name="spatial_gating",
    )(proj, proj, w_s, b_exp)


def _branch_kernel(a_ref, s_ref, wa_ref, ws_ref, ga_ref, gs_ref, o_ref):
    ya = jnp.dot(a_ref[...], wa_ref[...], preferred_element_type=F32)
    ys = jnp.dot(s_ref[...], ws_ref[...], preferred_element_type=F32)
    o_ref[...] = (ga_ref[...].astype(F32) * ya + gs_ref[...].astype(F32) * ys).astype(o_ref.dtype)


def _branch_mix(attn, sgu, w_ap, w_sp, proj, gate_col0, tm=1024, tn=512):
    S, K = attn.shape
    D = w_ap.shape[1]
    ga0 = gate_col0 // tn
    gs0 = (gate_col0 + D) // tn
    return pl.pallas_call(
        _branch_kernel,
        grid=(S // tm, D // tn),
        in_specs=[
            pl.BlockSpec((tm, K), lambda i, j: (i, 0)),
            pl.BlockSpec((tm, K), lambda i, j: (i, 0)),
            pl.BlockSpec((K, tn), lambda i, j: (0, j)),
            pl.BlockSpec((K, tn), lambda i, j: (0, j)),
            pl.BlockSpec((tm, tn), lambda i, j: (i, ga0 + j)),
            pl.BlockSpec((tm, tn), lambda i, j: (i, gs0 + j)),
        ],
        out_specs=pl.BlockSpec((tm, tn), lambda i, j: (i, j)),
        out_shape=jax.ShapeDtypeStruct((S, D), BF16),
        compiler_params=_params(("parallel", "arbitrary"),
                                2 * tm * K * 2 + 2 * K * tn * 2 + 3 * tm * tn * 2,
                                3 * tm * tn * 4),
        name="branch_mix",
    )(attn, sgu, w_ap, w_sp, proj, proj)


def _oproj_kernel(z_ref, w_ref, x_ref, o_ref):
    o_ref[...] = x_ref[...] + jnp.dot(z_ref[...], w_ref[...], preferred_element_type=F32)


def _out_proj_residual(z, w_o, x, tm=1024, tn=512):
    S, D = z.shape
    return pl.pallas_call(
        _oproj_kernel,
        grid=(S // tm, D // tn),
        in_specs=[
            pl.BlockSpec((tm, D), lambda i, j: (i, 0)),
            pl.BlockSpec((D, tn), lambda i, j: (0, j)),
            pl.BlockSpec((tm, tn), lambda i, j: (i, j)),
        ],
        out_specs=pl.BlockSpec((tm, tn), lambda i, j: (i, j)),
        out_shape=jax.ShapeDtypeStruct((S, D), F32),
        compiler_params=_params(("parallel", "arbitrary"),
                                tm * D * 2 + D * tn * 2 + 2 * tm * tn * 4,
                                2 * tm * tn * 4),
        name="out_proj_residual",
    )(z, w_o, x)


def _peer_select_kernel(h_ref, wq_ref, sk_ref, rb_ref, beta_ref, alpha_ref, lmap_ref, *, tt):
    K = PEER_TOPK
    qh = jnp.dot(h_ref[...], wq_ref[...], preferred_element_type=F32).astype(BF16)
    sa = lax.dot_general(sk_ref[0, 0], qh[:, :PEER_KEY_HALF], NT_DIMS, preferred_element_type=F32)
    sb = lax.dot_general(sk_ref[0, 1], qh[:, PEER_KEY_HALF:], NT_DIMS, preferred_element_type=F32)
    rowi = lax.broadcasted_iota(I32, (PEER_N_KEYS, tt), 0)
    rowk = lax.broadcasted_iota(I32, (K, tt), 0)

    def topk_rows(s):
        rank = jnp.full(s.shape, K, I32)
        srt = jnp.zeros((K, tt), F32)
        vals = []
        for r in range(K):
            m = jnp.max(s, axis=0, keepdims=True)
            idx = jnp.min(jnp.where(s == m, rowi, PEER_N_KEYS), axis=0, keepdims=True)
            hit = rowi == idx
            rank = jnp.where(hit, r, rank)
            s = jnp.where(hit, -jnp.inf, s)
            srt = jnp.where(rowk == r, m, srt)
            vals.append(m)
        return rank, srt, vals

    rank_a, srt_a, va = topk_rows(sa)
    rank_b, srt_b, vb = topk_rows(sb)

    half = K // 2
    row8 = lax.broadcasted_iota(I32, (half, tt), 0)
    groups = [va[0] + srt_b]
    for ka in range(1, half):
        groups.append(jnp.where(row8 < K // (ka + 1), va[ka] + srt_b[:half], -jnp.inf))
    groups.append(srt_a[half:] + vb[0])
    cand = jnp.concatenate(groups, axis=0)
    n_c = cand.shape[0]
    rowc = lax.broadcasted_iota(I32, (n_c, tt), 0)
    cur = cand
    win = jnp.zeros((n_c, tt), jnp.bool_)
    for _ in range(K):
        m = jnp.max(cur, axis=0, keepdims=True)
        idx = jnp.min(jnp.where(cur == m, rowc, n_c), axis=0, keepdims=True)
        hit = rowc == idx
        win = win | hit
        cur = jnp.where(hit, -jnp.inf, cur)
    winf = win.astype(F32)
    cmax = va[0] + vb[0]
    z = jnp.sum(jnp.where(win, jnp.exp(cand - cmax), 0.0), axis=0, keepdims=True)

    lmap = jnp.zeros((PEER_N_KEYS, tt), F32)
    lmap = jnp.where(rank_a == 0, jnp.sum(winf[:K], axis=0, keepdims=True), lmap)
    for ka in range(1, half):
        lo = K + (ka - 1) * half
        lmap = jnp.where(rank_a == ka, jnp.sum(winf[lo:lo + half], axis=0, keepdims=True), lmap)
    last = winf[n_c - half:]
    for r in range(half):
        lmap = jnp.where(rank_a == half + r, last[r:r + 1], lmap)

    rb_ref[0] = rank_b.astype(F32)
    beta_ref[0] = jnp.exp(sb - vb[0]) / z
    alpha_ref[0] = jnp.exp(sa - va[0])
    lmap_ref[0] = lmap


def _peer_select(h2, wq, subkeys, tt=256):
    S, D = h2.shape
    H = subkeys.shape[0]
    qw = 2 * PEER_KEY_HALF
    out = jax.ShapeDtypeStruct((H, PEER_N_KEYS, S), F32)
    ospec = pl.BlockSpec((1, PEER_N_KEYS, tt), lambda i, h: (h, 0, i))
    return pl.pallas_call(
        functools.partial(_peer_select_kernel, tt=tt),
        grid=(S // tt, H),
        in_specs=[
            pl.BlockSpec((tt, D), lambda i, h: (i, 0)),
            pl.BlockSpec((D, qw), lambda i, h: (0, h)),
            pl.BlockSpec((1, 2, PEER_N_KEYS, PEER_KEY_HALF), lambda i, h: (h, 0, 0, 0)),
        ],
        out_specs=[ospec] * 4,
        out_shape=[out] * 4,
        compiler_params=_params(("parallel", "arbitrary"),
                                tt * D * 2 + D * qw * 2 + 2 * PEER_N_KEYS * PEER_KEY_HALF * 2
                                + 4 * PEER_N_KEYS * tt * 4,
                                24 * PEER_N_KEYS * tt * 4),
        name="peer_select",
    )(h2, wq, subkeys)


def _peer_dense_kernel(h_ref, u_ref, v_ref, rb_ref, beta_ref, alpha_ref, lmap_ref, o_ref, *, te, heads):
    e = pl.program_id(1)

    @pl.when(e == 0)
    def _():
        o_ref[...] = jnp.zeros(o_ref.shape, F32)

    act = jax.nn.gelu(lax.dot_general(u_ref[...], h_ref[...], NT_DIMS, preferred_element_type=F32))
    per = te // PEER_N_KEYS
    parts = []
    for k in range(per):
        i0 = e * per + k
        g = None
        for h in range(heads):
            lrow = lmap_ref[h, pl.ds(i0, 1), :]
            arow = alpha_ref[h, pl.ds(i0, 1), :]
            term = jnp.where(rb_ref[h] < lrow, beta_ref[h], 0.0) * arow
            g = term if g is None else g + term
        parts.append(g * act[k * PEER_N_KEYS:(k + 1) * PEER_N_KEYS])
    p_t = jnp.concatenate(parts, axis=0) if per > 1 else parts[0]
    o_ref[...] += jnp.dot(p_t.T.astype(BF16), v_ref[...], preferred_element_type=F32)


def _peer_dense(h2, u_tab, v_tab, rb, beta, alpha, lmap, tt=512, te=512):
    S, D = h2.shape
    N = u_tab.shape[0]
    H = rb.shape[0]
    mspec = pl.BlockSpec((H, PEER_N_KEYS, tt), lambda i, e: (0, 0, i))
    return pl.pallas_call(
        functools.partial(_peer_dense_kernel, te=te, heads=H),
        grid=(S // tt, N // te),
        in_specs=[
            pl.BlockSpec((tt, D), lambda i, e: (i, 0)),
            pl.BlockSpec((te, D), lambda i, e: (e, 0)),
            pl.BlockSpec((te, D), lambda i, e: (e, 0)),
            mspec, mspec, mspec, mspec,
        ],
        out_specs=pl.BlockSpec((tt, D), lambda i, e: (i, 0)),
        out_shape=jax.ShapeDtypeStruct((S, D), F32),
        compiler_params=_params(("parallel", "arbitrary"),
                                tt * D * 2 + 2 * te * D * 2 + 4 * H * PEER_N_KEYS * tt * 4 + tt * D * 4,
                                6 * te * tt * 4),
        name="peer_dense",
    )(h2, u_tab, v_tab, rb, beta, alpha, lmap)


def _rope_tables(seq):
    half = HEAD_DIM // 2
    inv_freq = jnp.power(ROPE_THETA, -jnp.arange(half, dtype=F32) * (2.0 / HEAD_DIM))
    ang = jnp.arange(seq, dtype=F32)[:, None] * inv_freq[None, :]
    cos, sin = jnp.cos(ang), jnp.sin(ang)
    return jnp.concatenate([cos, cos], axis=-1), jnp.concatenate([-sin, sin], axis=-1)


def kernel(x, norm_mix_g, w_in, w_attn_proj, sgu_norm_g, sgu_w, sgu_b, w_sgu_proj, w_gate, b_gate, w_o,
           norm_ffn_g, peer_wq, peer_subkeys, peer_u, peer_v, norm_final_g):
    B, S, D = x.shape
    assert B == 1, "kernel is written for a single sequence"
    depth = w_in.shape[0]
    H = w_attn_proj.shape[1] // HEAD_DIM
    attn_w = H * HEAD_DIM
    sgu_w_total = w_sgu_proj.shape[1]
    widths = (attn_w, attn_w, attn_w, sgu_w_total, sgu_w_total, 2 * D)
    assert attn_w == sgu_w_total
    cosf, sinf = _rope_tables(S)
    nb = S // MOBA_BLOCK
    assert nb <= LANES

    xs = x.reshape(S, D)
    for l in range(depth):
        h = _rmsnorm([xs], norm_mix_g[l], BF16)
        w_cat = jnp.concatenate([w_in[l], w_gate[l]], axis=1).astype(BF16)
        colvec = jnp.concatenate([jnp.zeros((4 * attn_w,), F32), sgu_norm_g[l].reshape(-1),
                                  b_gate[l]]).reshape(1, -1)
        proj = _fused_proj(h, w_cat, colvec, cosf, sinf, widths, HEAD_DIM ** -0.5)

        kmean = _key_block_means(proj, 1, attn_w).reshape(nb, H, HEAD_DIM).transpose(1, 0, 2)
        kmean = jnp.pad(kmean, ((0, 0), (0, LANES - nb), (0, 0))).astype(BF16)
        attn = _moba_attention(proj, kmean, H)

        b_exp = jnp.repeat(sgu_b[l].T, SGU_GROUP_DIM, axis=1)
        sgu = _spatial_gating(proj, sgu_w[l], b_exp, 3, 4)

        z = _branch_mix(attn, sgu, w_attn_proj[l].astype(BF16), w_sgu_proj[l].astype(BF16), proj, 5 * attn_w)
        x1 = _out_proj_residual(z, w_o[l].astype(BF16), xs)

        h2 = _rmsnorm([x1], norm_ffn_g[l], BF16)
        rb, beta, alpha, lmap = _peer_select(h2, peer_wq[l].astype(BF16), peer_subkeys[l].astype(BF16))
        peer = _peer_dense(h2, peer_u[l].astype(BF16), peer_v[l].astype(BF16), rb, beta, alpha, lmap)
        if l + 1 < depth:
            xs = _residual_add(x1, peer)
    return _rmsnorm([x1, peer], norm_final_g, F32).reshape(B, S, D)
```
